```python
import math
import jax, jax.numpy as jnp
from jax import lax
import numpy as np

D_MODEL = 4096
BATCH = 4
SEQ = 4096
DEPTH = 4
DEC_BATCH = 32
DEC_SEQ = 16
PAST_LEN = 1024

CHUNK = 64
Q_BLOCK = 128
BRANCH_WIDTH = 1024
N_BRANCHES = 3

RW_HEADS = 16
RW_HEAD_DIM = 64
RW_DECAY_LORA = 64
RW_ICLR_LORA = 64
RW_GATE_LORA = 128
RW_GN_EPS = 64e-5
RW_COLS = 3 * BRANCH_WIDTH + RW_DECAY_LORA + RW_ICLR_LORA + RW_GATE_LORA
RW_SPLITS = (BRANCH_WIDTH, 2 * BRANCH_WIDTH, 3 * BRANCH_WIDTH,
             3 * BRANCH_WIDTH + RW_DECAY_LORA, 3 * BRANCH_WIDTH + RW_DECAY_LORA + RW_ICLR_LORA)

GD_QK_HEADS = 4
GD_V_HEADS = 8
GD_HEAD_DIM = 128
GD_CONV = 4
GD_QK = GD_QK_HEADS * GD_HEAD_DIM
GD_VW = GD_V_HEADS * GD_HEAD_DIM
GD_CONV_CH = 2 * GD_QK + GD_VW
GD_COLS = GD_CONV_CH + GD_VW + 2 * GD_V_HEADS
GD_SPLITS = (GD_CONV_CH, GD_CONV_CH + GD_VW, GD_CONV_CH + GD_VW + GD_V_HEADS)

ML_HEADS = 8
ML_Q_RANK = 1024
ML_KV_RANK = 512
ML_NOPE = 128
ML_ROPE = 64
ML_V = 128
ML_COLS = ML_Q_RANK + ML_KV_RANK + ML_ROPE
ROPE_THETA = 10000.0

GATE_COLS = N_BRANCHES * D_MODEL
IN_COLS = GATE_COLS + RW_COLS + GD_COLS + ML_COLS
IN_SPLITS = (GATE_COLS, GATE_COLS + RW_COLS, GATE_COLS + RW_COLS + GD_COLS)

N_EXPERTS = 64
TOP_K = 8
EXPERT_FF = 512
SHARED_FF = 512
ROUTED_SCALE = 2.5
MOE_BLOCK = 1024

DN_ALPHA = (2 * DEPTH) ** 0.25
DN_BETA = (8 * DEPTH) ** -0.25
LN_EPS = 1e-5
RMS_EPS = 1e-6
NEG_INF = -1e30

kernel_name = 'hybrid_streaming_encoder_step'


def layer_norm(x, g, b):
    xf = x.astype(jnp.float32)
    mu = jnp.mean(xf, -1, keepdims=True)
    var = jnp.mean(jnp.square(xf - mu), -1, keepdims=True)
    return ((xf - mu) * lax.rsqrt(var + LN_EPS) * g + b).astype(x.dtype)


def rms_norm(x, g):
    xf = x.astype(jnp.float32)
    return (xf * lax.rsqrt(jnp.mean(xf * xf, -1, keepdims=True) + RMS_EPS) * g).astype(x.dtype)


def l2_normalize(x):
    xf = x.astype(jnp.float32)
    return xf * lax.rsqrt(jnp.sum(xf * xf, -1, keepdims=True) + 1e-6)


def rope(x, pos):
    half = ML_ROPE // 2
    inv = ROPE_THETA ** (-jnp.arange(half, dtype=jnp.float32) / half)
    ang = pos.astype(jnp.float32)[:, None] * inv
    ang = ang.reshape(ang.shape[0], *([1] * (x.ndim - 3)), half)
    cos, sin = jnp.cos(ang), jnp.sin(ang)
    x1, x2 = jnp.split(x.astype(jnp.float32), 2, axis=-1)
    return jnp.concatenate([x1 * cos - x2 * sin, x1 * sin + x2 * cos], -1).astype(x.dtype)


def rwkv7_branch(p, shift_prev, S0, mu, w0, w2, a0, a2, g2, k_k, k_a, r_k, lnx_g, lnx_b):
    B, T, _ = p.shape
    pf = p.astype(jnp.float32)
    prev = jnp.concatenate([shift_prev.astype(jnp.float32), pf[:, :-1]], axis=1)
    pm = pf + (prev - pf) * mu
    r, k, v, pw, pa, pg = jnp.split(pm, RW_SPLITS, axis=-1)
    w_log = -jax.nn.softplus(-(w0 + jnp.tanh(pw) @ w2)) - 0.5
    decay = jnp.exp(-jnp.exp(w_log))
    a = jax.nn.sigmoid(a0 + pa @ a2)
    g = jax.nn.sigmoid(pg) @ g2
    hd = lambda t: t.reshape(B, T, RW_HEADS, RW_HEAD_DIM)
    kk = l2_normalize(hd(k * k_k))
    k = k * (1.0 + (a - 1.0) * k_a)
    r, k, v, a, decay = (hd(t) for t in (r, k, v, a, decay))

    def step(S, inp):
        r_t, w_t, k_t, v_t, kk_t, a_t = inp
        sa = jnp.einsum('bhij,bhj->bhi', S, -kk_t)
        S = (S * w_t[:, :, None, :] + sa[..., :, None] * (kk_t * a_t)[:, :, None, :]
             + v_t[..., :, None] * k_t[:, :, None, :])
        return S, jnp.einsum('bhij,bhj->bhi', S, r_t)

    xs = tuple(jnp.swapaxes(t, 0, 1) for t in (r, decay, k, v, kk, a))
    S_T, o = lax.scan(step, S0.astype(jnp.float32), xs)
    o = jnp.swapaxes(o, 0, 1)
    mean = jnp.mean(o, -1, keepdims=True)
    var = jnp.mean(jnp.square(o - mean), -1, keepdims=True)
    o = ((o - mean) * lax.rsqrt(var + RW_GN_EPS)).reshape(B, T, BRANCH_WIDTH) * lnx_g + lnx_b
    bonus = jnp.sum(r * k * r_k, -1, keepdims=True) * v
    o = (o + bonus.reshape(B, T, BRANCH_WIDTH)) * g
    return o.astype(p.dtype), p[:, -1:], S_T.astype(S0.dtype)


def causal_conv(xc, prev, w):
    T = xc.shape[1]
    xp = jnp.concatenate([prev, xc], axis=1)
    y = w[0] * xp[:, 0:T]
    for j in range(1, GD_CONV):
        y = y + w[j] * xp[:, j:j + T]
    return y, xp[:, -(GD_CONV - 1):]


def gated_delta_chunked(q, k, v, g, beta, S0):
    B, T, H, Dk = q.shape
    Dv = v.shape[-1]
    C = min(CHUNK, T)
    N = T // C
    chunks = lambda t: jnp.moveaxis(t.reshape(B, N, C, H, *t.shape[3:]), 3, 2)
    q, k, v, g, beta = (chunks(t) for t in (q, k, v, g, beta))
    gam = jnp.cumsum(g, axis=-1)
    idx = jnp.arange(C)
    incl = idx[:, None] >= idx[None, :]
    strict = idx[:, None] > idx[None, :]
    diff = gam[..., :, None] - gam[..., None, :]
    gmat = jnp.where(incl, jnp.exp(jnp.where(incl, diff, 0.0)), 0.0)
    kb = k * beta[..., None]
    L = jnp.where(strict, jnp.einsum('bnhid,bnhjd->bnhij', kb, k) * gmat, 0.0)
    A = L + jnp.eye(C, dtype=L.dtype)
    rhs = jnp.concatenate([v * beta[..., None], kb * jnp.exp(gam)[..., None]], axis=-1)
    sol = lax.linalg.triangular_solve(A, rhs, left_side=True, lower=True, unit_diagonal=True)
    U0, W = sol[..., :Dv], sol[..., Dv:]
    Aqk = jnp.einsum('bnhid,bnhjd->bnhij', q, k) * gmat
    qe = q * jnp.exp(gam)[..., None]
    kd = k * jnp.exp(gam[..., -1:] - gam)[..., None]
    gl = jnp.exp(gam[..., -1])

    def step(S, inp):
        u0, w_, aqk, qe_, kd_, gl_ = inp
        vn = u0 - jnp.einsum('bhcd,bhde->bhce', w_, S)
        o = jnp.einsum('bhcd,bhde->bhce', qe_, S) + jnp.einsum('bhij,bhje->bhie', aqk, vn)
        S = S * gl_[..., None, None] + jnp.einsum('bhcd,bhce->bhde', kd_, vn)
        return S, o

    xs = tuple(jnp.moveaxis(t, 1, 0) for t in (U0, W, Aqk, qe, kd, gl))
    S_T, o = lax.scan(step, S0, xs)
    o = jnp.transpose(o, (1, 0, 3, 2, 4)).reshape(B, T, H, Dv)
    return o, S_T


def gdn_branch(p, conv_prev, S0, conv_w, a_log, dt_bias, norm_g):
    B, T, _ = p.shape
    pf = p.astype(jnp.float32)
    qkv, z, b, a = jnp.split(pf, GD_SPLITS, axis=-1)
    qkv_c, conv_new = causal_conv(qkv, conv_prev.astype(jnp.float32), conv_w)
    qkv_c = jax.nn.silu(qkv_c)
    q, k, v = jnp.split(qkv_c, [GD_QK, 2 * GD_QK], axis=-1)
    rep = GD_V_HEADS // GD_QK_HEADS
    q = jnp.repeat(l2_normalize(q.reshape(B, T, GD_QK_HEADS, GD_HEAD_DIM)), rep, axis=2) * GD_HEAD_DIM ** -0.5
    k = jnp.repeat(l2_normalize(k.reshape(B, T, GD_QK_HEADS, GD_HEAD_DIM)), rep, axis=2)
    v = v.reshape(B, T, GD_V_HEADS, GD_HEAD_DIM)
    beta = jax.nn.sigmoid(b)
    g = -jnp.exp(a_log.astype(jnp.float32)) * jax.nn.softplus(a + dt_bias)
    o, S_T = gated_delta_chunked(q, k, v, g, beta, S0.astype(jnp.float32))
    o = rms_norm(o, norm_g) * jax.nn.silu(z.reshape(B, T, GD_V_HEADS, GD_HEAD_DIM))
    return (o.reshape(B, T, GD_VW).astype(p.dtype), conv_new.astype(conv_prev.dtype),
            S_T.astype(S0.dtype))


def chunk_attention(q, k, v, qpos, kpos):
    scale = (ML_NOPE + ML_ROPE) ** -0.5
    kchunk = kpos // CHUNK

    def block(qb, qpb):
        s = jnp.einsum('bqhd,bkhd->bhqk', qb, k).astype(jnp.float32) * scale
        mask = kchunk[None, :] <= (qpb // CHUNK)[:, None]
        pr = jax.nn.softmax(jnp.where(mask, s, NEG_INF), axis=-1).astype(v.dtype)
        return jnp.einsum('bhqk,bkhd->bqhd', pr, v)

    B, T = q.shape[:2]
    if T > Q_BLOCK and T % Q_BLOCK == 0:
        nb = T // Q_BLOCK
        qb = jnp.moveaxis(q.reshape(B, nb, Q_BLOCK, *q.shape[2:]), 1, 0)
        o = lax.map(lambda a: block(a[0], a[1]), (qb, qpos.reshape(nb, Q_BLOCK)))
        return jnp.moveaxis(o, 0, 1).reshape(B, T, *o.shape[3:])
    return block(q, qpos)


def mla_branch(p, pos, ckv_past, kpe_past, q_norm_g, w_uq, kv_norm_g, w_ukv):
    B, T, _ = p.shape
    cq, ckv, kpe = jnp.split(p, [ML_Q_RANK, ML_Q_RANK + ML_KV_RANK], axis=-1)
    q = (rms_norm(cq, q_norm_g) @ w_uq).reshape(B, T, ML_HEADS, ML_NOPE + ML_ROPE)
    q_nope, q_pe = jnp.split(q, [ML_NOPE], axis=-1)
    q = jnp.concatenate([q_nope, rope(q_pe, pos)], axis=-1)
    ckv = rms_norm(ckv, kv_norm_g)
    kpe = rope(kpe, pos)
    if ckv_past is None:
        ckv_all, kpe_all, kpos = ckv, kpe, pos
    else:
        P = ckv_past.shape[1]
        ckv_all = jnp.concatenate([ckv_past.astype(ckv.dtype), ckv], axis=1)
        kpe_all = jnp.concatenate([kpe_past.astype(kpe.dtype), kpe], axis=1)
        kpos = jnp.concatenate([jnp.arange(P, dtype=jnp.int32), pos])
    S = ckv_all.shape[1]
    kv = (ckv_all @ w_ukv).reshape(B, S, ML_HEADS, ML_NOPE + ML_V)
    k_nope, v = jnp.split(kv, [ML_NOPE], axis=-1)
    k = jnp.concatenate([k_nope, jnp.broadcast_to(kpe_all[:, :, None, :], (B, S, ML_HEADS, ML_ROPE))], axis=-1)
    o = chunk_attention(q, k, v, pos, kpos)
    return o.reshape(B, T, ML_HEADS * ML_V), ckv, kpe


def moe(x, w_router, router_bias, w_gate, w_up, w_down, ws_gate, ws_up, ws_down):
    B, T, D = x.shape
    n = B * T
    xt = x.reshape(n, D)
    scores = jax.nn.sigmoid((xt @ w_router).astype(jnp.float32))
    _, idx = lax.top_k(scores + router_bias.astype(jnp.float32), TOP_K)
    sel = jnp.take_along_axis(scores, idx, axis=-1)
    wts = sel / jnp.sum(sel, -1, keepdims=True) * ROUTED_SCALE
    gate = jnp.einsum('nk,nke->ne', wts, jax.nn.one_hot(idx, N_EXPERTS, dtype=jnp.float32))
    blk = MOE_BLOCK if n >= MOE_BLOCK else n
    pad = (-n) % blk
    xt_p = jnp.pad(xt, ((0, pad), (0, 0))).reshape(-1, blk, D)
    gate_p = jnp.pad(gate, ((0, pad), (0, 0))).reshape(-1, blk, N_EXPERTS)

    def block(args):
        xb, gb = args
        h = jax.nn.silu(jnp.einsum('td,edf->tef', xb, w_gate)) * jnp.einsum('td,edf->tef', xb, w_up)
        return jnp.einsum('tef,efd->td', h * gb[..., None].astype(h.dtype), w_down)

    routed = lax.map(block, (xt_p, gate_p)).reshape(-1, D)[:n]
    shared = (jax.nn.silu(xt @ ws_gate) * (xt @ ws_up)) @ ws_down
    return (routed + shared).reshape(B, T, D).astype(x.dtype)


def trunk_layer(x, pos, ckv_past, kpe_past, rw_S, rw_shift, gd_S, gd_conv,
                w_in, rw_mu, rw_w0, rw_w2, rw_a0, rw_a2, rw_g2, rw_k_k, rw_k_a, rw_r_k, rw_lnx_g, rw_lnx_b,
                gd_conv_w, gd_a_log, gd_dt_bias, gd_norm_g,
                ml_q_norm_g, ml_w_uq, ml_kv_norm_g, ml_w_ukv,
                w_branch, w_out, ln1_g, ln1_b,
                w_router, router_bias, w_gate, w_up, w_down, ws_gate, ws_up, ws_down, ln2_g, ln2_b):
    p = x @ w_in
    gates, p_rw, p_gd, p_ml = jnp.split(p, IN_SPLITS, axis=-1)
    o_rw, rw_shift_new, rw_S_new = rwkv7_branch(p_rw, rw_shift, rw_S, rw_mu, rw_w0, rw_w2, rw_a0, rw_a2,
                                                rw_g2, rw_k_k, rw_k_a, rw_r_k, rw_lnx_g, rw_lnx_b)
    o_gd, gd_conv_new, gd_S_new = gdn_branch(p_gd, gd_conv, gd_S, gd_conv_w, gd_a_log, gd_dt_bias, gd_norm_g)
    o_ml, ckv_new, kpe_new = mla_branch(p_ml, pos, ckv_past, kpe_past, ml_q_norm_g, ml_w_uq,
                                        ml_kv_norm_g, ml_w_ukv)
    g_rw, g_gd, g_ml = jnp.split(jax.nn.sigmoid(gates), N_BRANCHES, axis=-1)
    merged = (g_rw * (o_rw @ w_branch[0]) + g_gd * (o_gd @ w_branch[1])
              + g_ml * (o_ml @ w_branch[2]))
    x = layer_norm(DN_ALPHA * x + merged @ w_out, ln1_g, ln1_b)
    x = layer_norm(DN_ALPHA * x + moe(x, w_router, router_bias, w_gate, w_up, w_down, ws_gate, ws_up, ws_down),
                   ln2_g, ln2_b)
    return x, (ckv_new, kpe_new, rw_S_new, rw_shift_new, gd_S_new, gd_conv_new)


def setup_inputs(seed: int = 0) -> dict:
    key = jax.random.key(seed)
    ks = iter(jax.random.split(key, 64))

    def nrm(shape, scale=1.0):
        return jax.random.normal(next(ks), shape, jnp.float32) * scale

    def unif(shape, lo, hi):
        return jax.random.uniform(next(ks), shape, jnp.float32, lo, hi)

    L, D, W, E = DEPTH, D_MODEL, BRANCH_WIDTH, N_EXPERTS
    dt = jnp.exp(unif((L, GD_V_HEADS), math.log(1e-3), math.log(1e-1)))
    return {
        'x_prompt': nrm((BATCH, SEQ, D)),
        'x_sample': nrm((DEC_BATCH, DEC_SEQ, D)),
        'cache_ckv': nrm((L, DEC_BATCH, PAST_LEN, ML_KV_RANK)),
        'cache_kpe': nrm((L, DEC_BATCH, PAST_LEN, ML_ROPE)),
        'state_rwkv': nrm((L, DEC_BATCH, RW_HEADS, RW_HEAD_DIM, RW_HEAD_DIM), 0.2),
        'state_rwkv_shift': nrm((L, DEC_BATCH, 1, RW_COLS)),
        'state_gdn': nrm((L, DEC_BATCH, GD_V_HEADS, GD_HEAD_DIM, GD_HEAD_DIM), 0.1),
        'state_gdn_conv': nrm((L, DEC_BATCH, GD_CONV - 1, GD_CONV_CH)),
        'w_in': nrm((L, D, IN_COLS), D ** -0.5),
        'rw_mu': unif((L, RW_COLS), 0.0, 1.0),
        'rw_w0': unif((L, W), -6.0, -1.0),
        'rw_w2': nrm((L, RW_DECAY_LORA, W), 0.1),
        'rw_a0': nrm((L, W), 0.1),
        'rw_a2': nrm((L, RW_ICLR_LORA, W), RW_ICLR_LORA ** -0.5),
        'rw_g2': nrm((L, RW_GATE_LORA, W), RW_GATE_LORA ** -0.5),
        'rw_k_k': 0.85 + nrm((L, W), 0.02),
        'rw_k_a': 1.0 + nrm((L, W), 0.02),
        'rw_r_k': nrm((L, RW_HEADS, RW_HEAD_DIM), 0.1),
        'rw_lnx_g': 1.0 + nrm((L, W), 0.02),
        'rw_lnx_b': nrm((L, W), 0.02),
        'gd_conv_w': nrm((L, GD_CONV, GD_CONV_CH), 0.5),
        'gd_a_log': jnp.log(unif((L, GD_V_HEADS), 1.0, 16.0)),
        'gd_dt_bias': jnp.log(jnp.expm1(dt)),
        'gd_norm_g': 1.0 + nrm((L, GD_HEAD_DIM), 0.02),
        'ml_q_norm_g': 1.0 + nrm((L, ML_Q_RANK), 0.02),
        'ml_w_uq': nrm((L, ML_Q_RANK, ML_HEADS * (ML_NOPE + ML_ROPE)), ML_Q_RANK ** -0.5),
        'ml_kv_norm_g': 1.0 + nrm((L, ML_KV_RANK), 0.02),
        'ml_w_ukv': nrm((L, ML_KV_RANK, ML_HEADS * (ML_NOPE + ML_V)), ML_KV_RANK ** -0.5),
        'w_branch': nrm((L, N_BRANCHES, W, D), W ** -0.5),
        'w_out': nrm((L, D, D), D ** -0.5 * DN_BETA),
        'ln1_g': 1.0 + nrm((L, D), 0.02),
        'ln1_b': nrm((L, D), 0.02),
        'w_router': nrm((L, D, E), D ** -0.5),
        'router_bias': nrm((L, E), 0.01),
        'w_gate': nrm((L, E, D, EXPERT_FF), D ** -0.5),
        'w_up': nrm((L, E, D, EXPERT_FF), D ** -0.5),
        'w_down': nrm((L, E, EXPERT_FF, D), EXPERT_FF ** -0.5 * DN_BETA),
        'ws_gate': nrm((L, D, SHARED_FF), D ** -0.5),
        'ws_up': nrm((L, D, SHARED_FF), D ** -0.5),
        'ws_down': nrm((L, SHARED_FF, D), SHARED_FF ** -0.5 * DN_BETA),
        'ln2_g': 1.0 + nrm((L, D), 0.02),
        'ln2_b': nrm((L, D), 0.02),
    }


def reference(x_prompt, x_sample, cache_ckv, cache_kpe, state_rwkv, state_rwkv_shift, state_gdn, state_gdn_conv,
              w_in, rw_mu, rw_w0, rw_w2, rw_a0, rw_a2, rw_g2, rw_k_k, rw_k_a, rw_r_k, rw_lnx_g, rw_lnx_b,
              gd_conv_w, gd_a_log, gd_dt_bias, gd_norm_g,
              ml_q_norm_g, ml_w_uq, ml_kv_norm_g, ml_w_ukv,
              w_branch, w_out, ln1_g, ln1_b,
              w_router, router_bias, w_gate, w_up, w_down, ws_gate, ws_up, ws_down, ln2_g, ln2_b):
    layer_weights = (w_in, rw_mu, rw_w0, rw_w2, rw_a0, rw_a2, rw_g2, rw_k_k, rw_k_a, rw_r_k, rw_lnx_g, rw_lnx_b,
                     gd_conv_w, gd_a_log, gd_dt_bias, gd_norm_g,
                     ml_q_norm_g, ml_w_uq, ml_kv_norm_g, ml_w_ukv,
                     w_branch, w_out, ln1_g, ln1_b,
                     w_router, router_bias, w_gate, w_up, w_down, ws_gate, ws_up, ws_down, ln2_g, ln2_b)
    Bp, Tp, _ = x_prompt.shape
    Ts = x_sample.shape[1]
    P = cache_ckv.shape[2]
    dt = x_prompt.dtype
    pos_p = jnp.arange(Tp, dtype=jnp.int32)
    pos_s = P + jnp.arange(Ts, dtype=jnp.int32)
    rw_S0 = jnp.zeros((Bp, RW_HEADS, RW_HEAD_DIM, RW_HEAD_DIM), dt)
    rw_sh0 = jnp.zeros((Bp, 1, RW_COLS), dt)
    gd_S0 = jnp.zeros((Bp, GD_V_HEADS, GD_HEAD_DIM, GD_HEAD_DIM), dt)
    gd_c0 = jnp.zeros((Bp, GD_CONV - 1, GD_CONV_CH), dt)
    xp, xs = x_prompt, x_sample
    prompt_states, sample_states = [], []
    for l in range(DEPTH):
        lw = [w[l] for w in layer_weights]
        xp, sp = trunk_layer(xp, pos_p, None, None, rw_S0, rw_sh0, gd_S0, gd_c0, *lw)
        xs, ss = trunk_layer(xs, pos_s, cache_ckv[l], cache_kpe[l], state_rwkv[l], state_rwkv_shift[l],
                             state_gdn[l], state_gdn_conv[l], *lw)
        prompt_states.append(sp)
        sample_states.append(ss)
    ckv_p, kpe_p, rw_p, rwsh_p, gd_p, gdc_p = (jnp.stack(t) for t in zip(*prompt_states))
    ckv_s, kpe_s, rw_s, rwsh_s, gd_s, gdc_s = (jnp.stack(t) for t in zip(*sample_states))
    return (xp, xs, ckv_p, kpe_p, rw_p, rwsh_p, gd_p, gdc_p, ckv_s, kpe_s, rw_s, rwsh_s, gd_s, gdc_s)
```

```python
import functools
import math

import jax
import jax.numpy as jnp
from jax import lax
from jax.experimental import pallas as pl
from jax.experimental.pallas import tpu as pltpu

F32 = jnp.float32
BF16 = jnp.bfloat16

D_MODEL = 4096
CHUNK = 64
BRANCH_WIDTH = 1024
N_BRANCHES = 3
RW_HEADS = 16
RW_HEAD_DIM = 64
RW_DECAY_LORA = 64
RW_ICLR_LORA = 64
RW_GATE_LORA = 128
RW_GN_EPS = 64e-5
RW_COLS = 3 * BRANCH_WIDTH + RW_DECAY_LORA + RW_ICLR_LORA + RW_GATE_LORA
GD_QK_HEADS = 4
GD_V_HEADS = 8
GD_HEAD_DIM = 128
GD_CONV = 4
GD_QK = GD_QK_HEADS * GD_HEAD_DIM
GD_VW = GD_V_HEADS * GD_HEAD_DIM
GD_CONV_CH = 2 * GD_QK + GD_VW
GD_COLS = GD_CONV_CH + GD_VW + 2 * GD_V_HEADS
ML_HEADS = 8
ML_Q_RANK = 1024
ML_KV_RANK = 512
ML_NOPE = 128
ML_ROPE = 64
ML_V = 128
ML_COLS = ML_Q_RANK + ML_KV_RANK + ML_ROPE
ROPE_THETA = 10000.0
GATE_COLS = N_BRANCHES * D_MODEL
IN_COLS = GATE_COLS + RW_COLS + GD_COLS + ML_COLS
N_EXPERTS = 64
TOP_K = 8
EXPERT_FF = 512
SHARED_FF = 512
ROUTED_SCALE = 2.5
LN_EPS = 1e-5
RMS_EPS = 1e-6
NEG_INF = -1e30

VMEM_LIMIT_BYTES = 56 * 1024 * 1024
INV_BLOCK = 16
MOE_TILE = 512


def _params(*sem):
    return pltpu.CompilerParams(dimension_semantics=sem, vmem_limit_bytes=VMEM_LIMIT_BYTES)


def _dot(a, b):
    return jnp.dot(a.astype(BF16), b.astype(BF16), preferred_element_type=F32)


def _dot_nt(a, b):
    return lax.dot_general(a.astype(BF16), b.astype(BF16), (((1,), (1,)), ((), ())),
                           preferred_element_type=F32)


def _dot_tn(a, b):
    return lax.dot_general(a.astype(BF16), b.astype(BF16), (((0,), (0,)), ((), ())),
                           preferred_element_type=F32)


def _split3(a):
    h = a.astype(BF16)
    r = a - h.astype(F32)
    m = r.astype(BF16)
    l = (r - m.astype(F32)).astype(BF16)
    return h, m, l


def _dot_exact_rhs(a, b_bf16):
    h, m, l = _split3(a)
    d = lambda x: jnp.dot(x, b_bf16, preferred_element_type=F32)
    return d(h) + d(m) + d(l)


def _dot_exact_lhs(a_bf16, b):
    h, m, l = _split3(b)
    d = lambda x: jnp.dot(a_bf16, x, preferred_element_type=F32)
    return d(h) + d(m) + d(l)


def _dot_hi(a, b):
    ah = a.astype(BF16)
    al = (a - ah.astype(F32)).astype(BF16)
    bh = b.astype(BF16)
    bl = (b - bh.astype(F32)).astype(BF16)
    d = lambda x, y: jnp.dot(x, y, preferred_element_type=F32)
    return d(ah, bh) + d(ah, bl) + d(al, bh)


def _iota2(c, d=None):
    d = c if d is None else d
    return (lax.broadcasted_iota(jnp.int32, (c, d), 0), lax.broadcasted_iota(jnp.int32, (c, d), 1))


def _unit_lower_inverse(L, c):
    row, col = _iota2(c)
    eye = (row == col).astype(F32)
    same = (row // INV_BLOCK) == (col // INV_BLOCK)
    Ld = jnp.where(same, L, 0.0)
    x = eye - Ld
    p = _dot_hi(Ld, Ld)
    x = x + _dot_hi(x, p)
    p = _dot_hi(p, p)
    x = x + _dot_hi(x, p)
    p = _dot_hi(p, p)
    x = x + _dot_hi(x, p)
    if c == INV_BLOCK:
        return x
    assert c <= 4 * INV_BLOCK
    Lo = L - Ld
    m = _dot_hi(x, Lo)
    m2 = _dot_hi(m, m)
    y = (eye - m) + _dot_hi(eye - m, m2)
    return _dot_hi(y, x)


def _mm_kernel(x_ref, w_ref, o_ref, *, nk):
    d = _dot(x_ref[...], w_ref[...])
    if nk == 1:
        o_ref[...] = d
    else:
        k = pl.program_id(2)

        @pl.when(k == 0)
        def _():
            o_ref[...] = d

        @pl.when(k > 0)
        def _():
            o_ref[...] += d


def _mm(x, w, layer=None, *, tm, tn, tk, xk0=0):
    M = x.shape[0]
    K, N = w.shape[-2:]
    tm = min(tm, M)
    tn = min(tn, N)
    tk = min(tk, K)
    assert K % tk == 0
    nk = K // tk
    if w.ndim == 3:
        w_spec = pl.BlockSpec((None, tk, tn), lambda i, j, k: (layer, k, j))
    else:
        w_spec = pl.BlockSpec((tk, tn), lambda i, j, k: (k, j))
    return pl.pallas_call(
        functools.partial(_mm_kernel, nk=nk),
        out_shape=jax.ShapeDtypeStruct((M, N), F32),
        grid=(pl.cdiv(M, tm), pl.cdiv(N, tn), nk),
        in_specs=[pl.BlockSpec((tm, tk), lambda i, j, k: (i, k + xk0)), w_spec],
        out_specs=pl.BlockSpec((tm, tn), lambda i, j, k: (i, j)),
        compiler_params=_params("parallel", "parallel", "arbitrary"),
        name="mm",
    )(x, w)


def _ln_kernel(x_ref, r_ref, g_ref, b_ref, o_ref, o16_ref, *, alpha):
    z = alpha * x_ref[...] + r_ref[...]
    mu = jnp.mean(z, -1, keepdims=True)
    zc = z - mu
    var = jnp.mean(zc * zc, -1, keepdims=True)
    y = zc * lax.rsqrt(var + LN_EPS) * g_ref[...] + b_ref[...]
    o_ref[...] = y
    o16_ref[...] = y.astype(BF16)


def _add_ln(x, r, g, b, alpha, *, tm=256):
    M, D = x.shape
    tm = min(tm, M)
    row = pl.BlockSpec((tm, D), lambda i: (i, 0))
    vec = pl.BlockSpec((1, D), lambda i: (0, 0))
    return pl.pallas_call(
        functools.partial(_ln_kernel, alpha=alpha),
        out_shape=(jax.ShapeDtypeStruct((M, D), F32), jax.ShapeDtypeStruct((M, D), BF16)),
        grid=(pl.cdiv(M, tm),),
        in_specs=[row, row, vec, vec],
        out_specs=(row, row),
        compiler_params=_params("parallel"),
        name="add_ln",
    )(x, r, g.reshape(1, D), b.reshape(1, D))


def _merge_kernel(g0_ref, g1_ref, g2_ref, o0_ref, o1_ref, o2_ref, w_ref, out_ref):
    acc = jax.nn.sigmoid(g0_ref[...]) * _dot(o0_ref[...], w_ref[0])
    acc += jax.nn.sigmoid(g1_ref[...]) * _dot(o1_ref[...], w_ref[1])
    acc += jax.nn.sigmoid(g2_ref[...]) * _dot(o2_ref[...], w_ref[2])
    out_ref[...] = acc


def _merge(p, o_rw, o_gd, o_ml, w_branch, layer, *, tm=512, tn=512):
    M = p.shape[0]
    D = D_MODEL
    W = BRANCH_WIDTH
    tm = min(tm, M)
    nb = D // tn
    gate = lambda b: pl.BlockSpec((tm, tn), lambda i, j: (i, b * nb + j))
    o_spec = pl.BlockSpec((tm, W), lambda i, j: (i, 0))
    return pl.pallas_call(
        _merge_kernel,
        out_shape=jax.ShapeDtypeStruct((M, D), F32),
        grid=(pl.cdiv(M, tm), nb),
        in_specs=[gate(0), gate(1), gate(2), o_spec, o_spec, o_spec,
                  pl.BlockSpec((None, N_BRANCHES, W, tn), lambda i, j: (layer, 0, 0, j))],
        out_specs=pl.BlockSpec((tm, tn), lambda i, j: (i, j)),
        compiler_params=_params("parallel", "parallel"),
        name="branch_merge",
    )(p, p, p, o_rw, o_gd, o_ml, w_branch)


def _router_kernel(x_ref, w_ref, b_ref, idx_ref, wts_ref):
    logits = _dot_hi(x_ref[...], w_ref[...])
    scores = jax.nn.sigmoid(logits)
    s = scores + b_ref[...]
    tm, E = s.shape
    lane = lax.broadcasted_iota(jnp.int32, (tm, E), 1)
    kcol = lax.broadcasted_iota(jnp.int32, (tm, TOP_K), 1)
    idx = jnp.zeros((tm, TOP_K), jnp.int32)
    sel = jnp.zeros((tm, TOP_K), F32)
    for r in range(TOP_K):
        m = jnp.max(s, -1, keepdims=True)
        pick = jnp.min(jnp.where(s == m, lane, E), -1, keepdims=True)
        hit = lane == pick
        val = jnp.sum(jnp.where(hit, scores, 0.0), -1, keepdims=True)
        idx = jnp.where(kcol == r, pick, idx)
        sel = jnp.where(kcol == r, val, sel)
        s = jnp.where(hit, -jnp.inf, s)
    idx_ref[...] = idx
    wts_ref[...] = sel / jnp.sum(sel, -1, keepdims=True) * ROUTED_SCALE


def _router(x, w_router, router_bias, layer, *, tm=512):
    M, D = x.shape
    E = N_EXPERTS
    tm = min(tm, M)
    return pl.pallas_call(
        _router_kernel,
        out_shape=(jax.ShapeDtypeStruct((M, TOP_K), jnp.int32), jax.ShapeDtypeStruct((M, TOP_K), F32)),
        grid=(pl.cdiv(M, tm),),
        in_specs=[pl.BlockSpec((tm, D), lambda i: (i, 0)),
                  pl.BlockSpec((None, D, E), lambda i: (layer, 0, 0)),
                  pl.BlockSpec((None, 1, E), lambda i: (layer, 0, 0))],
        out_specs=(pl.BlockSpec((tm, TOP_K), lambda i: (i, 0)), pl.BlockSpec((tm, TOP_K), lambda i: (i, 0))),
        compiler_params=_params("parallel"),
        name="router",
    )(x, w_router, router_bias.reshape(router_bias.shape[0], 1, E))


def _expert_up_kernel(te_ref, nu_ref, x_ref, wg_ref, wu_ref, rw_ref, h_ref):
    t = pl.program_id(0)

    @pl.when(t < nu_ref[0])
    def _():
        x = x_ref[...]
        g = _dot(x, wg_ref[...])
        u = _dot(x, wu_ref[...])
        h_ref[...] = (jax.nn.silu(g) * u * rw_ref[...]).astype(h_ref.dtype)


def _expert_down_kernel(te_ref, nu_ref, h_ref, wd_ref, y_ref):
    t = pl.program_id(0)

    @pl.when(t < nu_ref[0])
    def _():
        y_ref[...] = _dot(h_ref[...], wd_ref[...])


def _experts(xs, row_w, tile_e, n_used, w_gate, w_up, w_down, layer, tm):
    R, D = xs.shape
    F = EXPERT_FF
    nt = R // tm
    h = pl.pallas_call(
        _expert_up_kernel,
        out_shape=jax.ShapeDtypeStruct((R, F), BF16),
        grid_spec=pltpu.PrefetchScalarGridSpec(
            num_scalar_prefetch=2,
            grid=(nt,),
            in_specs=[pl.BlockSpec((tm, D), lambda t, te, nu: (t, 0)),
                      pl.BlockSpec((None, None, D, F), lambda t, te, nu: (layer, te[t], 0, 0)),
                      pl.BlockSpec((None, None, D, F), lambda t, te, nu: (layer, te[t], 0, 0)),
                      pl.BlockSpec((tm, 1), lambda t, te, nu: (t, 0))],
            out_specs=pl.BlockSpec((tm, F), lambda t, te, nu: (t, 0)),
        ),
        compiler_params=_params("arbitrary"),
        name="expert_up",
    )(tile_e, n_used, xs, w_gate, w_up, row_w)
    return pl.pallas_call(
        _expert_down_kernel,
        out_shape=jax.ShapeDtypeStruct((R, D), F32),
        grid_spec=pltpu.PrefetchScalarGridSpec(
            num_scalar_prefetch=2,
            grid=(nt,),
            in_specs=[pl.BlockSpec((tm, F), lambda t, te, nu: (t, 0)),
                      pl.BlockSpec((None, None, F, D), lambda t, te, nu: (layer, te[t], 0, 0))],
            out_specs=pl.BlockSpec((tm, D), lambda t, te, nu: (t, 0)),
        ),
        compiler_params=_params("arbitrary"),
        name="expert_down",
    )(tile_e, n_used, h, w_down)


def _shared_act_kernel(gu_ref, o_ref):
    F = SHARED_FF
    gu = gu_ref[...]
    o_ref[...] = jax.nn.silu(gu[:, :F]) * gu[:, F:]


def _moe(x, x16, w_router, router_bias, w_gate, w_up, w_down, ws_gu, ws_down, layer):
    n, D = x.shape
    E = N_EXPERTS
    tm = min(MOE_TILE, n)
    idx, wts = _router(x, w_router, router_bias, layer)
    P = n * TOP_K
    e_flat = idx.reshape(P)
    order = jnp.argsort(e_flat, stable=True).astype(jnp.int32)
    inv = jnp.argsort(order).astype(jnp.int32)
    sorted_e = e_flat[order]
    bounds = jnp.searchsorted(sorted_e, jnp.arange(E + 1, dtype=jnp.int32), side="left").astype(jnp.int32)
    start = bounds[:-1]
    counts = bounds[1:] - start
    padded = ((counts + tm - 1) // tm) * tm
    pend = jnp.cumsum(padded)
    pstart = pend - padded
    R = (P // tm + E) * tm
    n_used = (pend[-1] // tm).astype(jnp.int32)
    tile_start = jnp.arange(R // tm, dtype=jnp.int32) * tm
    tile_e = jnp.searchsorted(pend, jnp.minimum(tile_start, pend[-1] - 1), side="right").astype(jnp.int32)
    tile_e = jnp.minimum(tile_e, E - 1)
    row_e = jnp.repeat(tile_e, tm)
    rank = jnp.arange(R, dtype=jnp.int32) - pstart[row_e]
    valid = rank < counts[row_e]
    src_pair = order[jnp.clip(start[row_e] + rank, 0, P - 1)]
    row_token = jnp.where(valid, src_pair // TOP_K, 0)
    row_w = jnp.where(valid, wts.reshape(P)[src_pair], 0.0).reshape(R, 1)
    dest = pstart[sorted_e] + (jnp.arange(P, dtype=jnp.int32) - start[sorted_e])
    pos = dest[inv].reshape(n, TOP_K)
    xs = jnp.take(x16, row_token, axis=0)
    ys = _experts(xs, row_w, tile_e, n_used.reshape(1), w_gate, w_up, w_down, layer, tm)
    routed = jnp.sum(jnp.take(ys, pos, axis=0), axis=1)
    gu = _mm(x16, ws_gu, layer, tm=1536, tn=1024, tk=1024)
    hs = pl.pallas_call(
        _shared_act_kernel,
        out_shape=jax.ShapeDtypeStruct((n, SHARED_FF), F32),
        grid=(pl.cdiv(n, 512),),
        in_specs=[pl.BlockSpec((min(512, n), 2 * SHARED_FF), lambda i: (i, 0))],
        out_specs=pl.BlockSpec((min(512, n), SHARED_FF), lambda i: (i, 0)),
        compiler_params=_params("parallel"),
        name="shared_act",
    )(gu)
    shared = _mm(hs, ws_down, layer, tm=1536, tn=1024, tk=512)
    return routed + shared


def _gdn_kernel(q_ref, k_ref, v_ref, g_ref, b_ref, s0_ref, o_ref, sout_ref, s_sc, *, c, nchunk):
    t = pl.program_id(2)
    dk = GD_HEAD_DIM

    @pl.when(t == 0)
    def _():
        s_sc[...] = s0_ref[...]

    row, col = _iota2(c)
    incl = row >= col
    strict = row > col
    tri = incl.astype(F32)
    eye = (row == col).astype(F32)
    r2, c2 = _iota2(c, dk + c)
    rhs = jnp.where(c2 < dk, 1.0, (r2 > c2 - dk).astype(F32)).astype(BF16)

    def chunk(ci, carry):
        r0 = pl.multiple_of(ci * c, c)
        kc = k_ref[pl.ds(r0, c), :]
        qc = q_ref[pl.ds(r0, c), :]
        kk = _dot_nt(kc, kc)
        qk = _dot_nt(qc, kc)
        for hh in range(2):
            g_row = g_ref[hh, ci]
            b_row = b_ref[hh, ci]
            lhs = jnp.concatenate([tri * g_row, eye * b_row], axis=0)
            cs = _dot_exact_rhs(lhs, rhs)
            gc = cs[:c, :dk]
            dmat = cs[:c, dk:]
            bc = cs[c:, :dk]
            gmat = jnp.where(incl, jnp.exp(jnp.where(incl, dmat, 0.0)), 0.0)
            L = jnp.where(strict, kk * bc[:, :c] * gmat, 0.0)
            T = _unit_lower_inverse(L, c)
            aqk = qk * gmat
            vc = v_ref[pl.ds(r0, c), hh * dk:(hh + 1) * dk]
            eg = jnp.exp(gc)
            glast = gc[c - 1:c, :]
            S = s_sc[hh]
            kq = _dot(jnp.concatenate([kc * bc * eg, qc * eg], 0), S)
            vn = _dot(T, vc * bc - kq[:c])
            o = kq[c:] + _dot(aqk, vn)
            o_ref[pl.ds(r0, c), hh * dk:(hh + 1) * dk] = o
            s_sc[hh] = S * jnp.exp(glast) + _dot_tn(kc * jnp.exp(glast - gc), vn)
        return carry

    lax.fori_loop(0, nchunk, chunk, 0)

    @pl.when(t == pl.num_programs(2) - 1)
    def _():
        sout_ref[...] = s_sc[...]


def _gdn_scan(q, k, v, g, beta, S0):
    B, T, _ = q.shape
    c = min(CHUNK, T)
    tb = min(256, T)
    nchunk = tb // c
    H = GD_V_HEADS
    dk = GD_HEAD_DIM
    rows = lambda a: jnp.swapaxes(a, 1, 2).reshape(B, H, T // c, 1, c)
    qk_spec = pl.BlockSpec((None, tb, dk), lambda b, h, t: (b, t, h))
    v_spec = pl.BlockSpec((None, tb, 2 * dk), lambda b, h, t: (b, t, h))
    r_spec = pl.BlockSpec((None, 2, nchunk, 1, c), lambda b, h, t: (b, h, t, 0, 0))
    s_spec = pl.BlockSpec((None, 2, dk, dk), lambda b, h, t: (b, h, 0, 0))
    return pl.pallas_call(
        functools.partial(_gdn_kernel, c=c, nchunk=nchunk),
        out_shape=(jax.ShapeDtypeStruct((B, T, H * dk), F32), jax.ShapeDtypeStruct(S0.shape, F32)),
        grid=(B, GD_QK_HEADS, T // tb),
        in_specs=[qk_spec, qk_spec, v_spec, r_spec, r_spec, s_spec],
        out_specs=(v_spec, s_spec),
        scratch_shapes=[pltpu.VMEM((2, dk, dk), F32)],
        compiler_params=_params("parallel", "parallel", "arbitrary"),
        name="gdn_scan",
    )(q, k, v, rows(g), rows(beta), S0)


def _rwkv_kernel(r_ref, lw_ref, k_ref, v_ref, kk_ref, a_ref, s0_ref, o_ref, sout_ref, s_sc, *, c, nchunk):
    t = pl.program_id(2)
    n = RW_HEAD_DIM

    @pl.when(t == 0)
    def _():
        s_sc[...] = s0_ref[...]

    row, col = _iota2(c)
    incl = row >= col
    strict = row > col
    tri = incl.astype(BF16)
    row2, col2 = _iota2(c, 2 * c)
    incl2 = row2 >= jnp.where(col2 >= c, col2 - c, col2)

    def chunk(ci, carry):
        sl = pl.ds(pl.multiple_of(ci * c, c), c)
        lw = lw_ref[sl, :]
        kk = kk_ref[sl, :]
        ka = kk * a_ref[sl, :]
        kc = k_ref[sl, :]
        cum = _dot_exact_lhs(tri, lw)
        cl = cum[c - 1:c, :]
        winv = jnp.exp(-cum)
        wrem = jnp.exp(cl - cum)
        a_t = -kk * jnp.exp(cum - lw)
        r_t = r_ref[sl, :] * jnp.exp(cum)
        b_t = ka * winv
        k_t = kc * winv
        b_h = ka * wrem
        k_h = kc * wrem
        wc = jnp.exp(cl)
        vc = v_ref[sl, :]
        outs = []
        for hh in range(2):
            hs = slice(hh * n, (hh + 1) * n)
            ar = jnp.concatenate([a_t[:, hs], r_t[:, hs]], 0)
            p = _dot_nt(ar, jnp.concatenate([b_t[:, hs], k_t[:, hs]], 0))
            aab = jnp.where(strict, p[:c, :c], 0.0)
            aak = jnp.where(strict, p[:c, c:], 0.0)
            arbk = jnp.where(incl2, p[c:, :], 0.0)
            T = _unit_lower_inverse(-aab, c)
            S = s_sc[hh]
            vh = vc[:, hs]
            ars = _dot_nt(ar, S)
            u = _dot(T, ars[:c] + _dot(aak, vh))
            uv = jnp.concatenate([u, vh], 0)
            outs.append(ars[c:] + _dot(arbk, uv))
            s_sc[hh] = S * wc[:, hs] + _dot_tn(uv, jnp.concatenate([b_h[:, hs], k_h[:, hs]], 0))
        o_ref[sl, :] = jnp.concatenate(outs, axis=1)
        return carry

    lax.fori_loop(0, nchunk, chunk, 0)

    @pl.when(t == pl.num_programs(2) - 1)
    def _():
        sout_ref[...] = s_sc[...]


def _rwkv_scan(r, lw, k, v, kk, a, S0):
    B, T, W = r.shape
    c = min(CHUNK, T)
    tb = min(256, T)
    nchunk = tb // c
    n = RW_HEAD_DIM
    x_spec = pl.BlockSpec((None, tb, 2 * n), lambda b, h, t: (b, t, h))
    s_spec = pl.BlockSpec((None, 2, n, n), lambda b, h, t: (b, h, 0, 0))
    return pl.pallas_call(
        functools.partial(_rwkv_kernel, c=c, nchunk=nchunk),
        out_shape=(jax.ShapeDtypeStruct((B, T, W), F32), jax.ShapeDtypeStruct(S0.shape, F32)),
        grid=(B, RW_HEADS // 2, T // tb),
        in_specs=[x_spec] * 6 + [s_spec],
        out_specs=(x_spec, s_spec),
        scratch_shapes=[pltpu.VMEM((2, n, n), F32)],
        compiler_params=_params("parallel", "parallel", "arbitrary"),
        name="rwkv_scan",
    )(r, lw, k, v, kk, a, S0)


def _attn_kernel(q_ref, kv_ref, kpe_ref, o_ref, m_sc, l_sc, acc_sc, *, tq, tk, q_off, scale):
    qi = pl.program_id(1)
    ki = pl.program_id(2)
    nk = pl.num_programs(2)

    @pl.when(ki == 0)
    def _():
        m_sc[...] = jnp.full(m_sc.shape, NEG_INF, F32)
        l_sc[...] = jnp.zeros(l_sc.shape, F32)
        acc_sc[...] = jnp.zeros(acc_sc.shape, F32)

    last_q_chunk = (q_off + (qi + 1) * tq - 1) // CHUNK

    @pl.when(ki * tk <= last_q_chunk * CHUNK + CHUNK - 1)
    def _():
        qpos = q_off + qi * tq + lax.broadcasted_iota(jnp.int32, (tq, tk), 0)
        kpos = ki * tk + lax.broadcasted_iota(jnp.int32, (tq, tk), 1)
        mask = (kpos // CHUNK) <= (qpos // CHUNK)
        kpe = kpe_ref[...]
        for h in range(ML_HEADS):
            qn = q_ref[:, h * ML_NOPE:(h + 1) * ML_NOPE]
            qp = q_ref[:, ML_HEADS * ML_NOPE + h * ML_ROPE: ML_HEADS * ML_NOPE + (h + 1) * ML_ROPE]
            kn = kv_ref[:, h * (ML_NOPE + ML_V): h * (ML_NOPE + ML_V) + ML_NOPE]
            vv = kv_ref[:, h * (ML_NOPE + ML_V) + ML_NOPE:(h + 1) * (ML_NOPE + ML_V)]
            s = (_dot_nt(qn, kn) + _dot_nt(qp, kpe)) * scale
            s = jnp.where(mask, s, NEG_INF)
            m_old = m_sc[h]
            m_new = jnp.maximum(m_old, jnp.max(s, -1, keepdims=True))
            alpha = jnp.exp(m_old - m_new)
            p = jnp.exp(s - m_new)
            l_sc[h] = alpha * l_sc[h] + jnp.sum(p, -1, keepdims=True)
            acc_sc[h] = alpha * acc_sc[h] + _dot(p, vv)
            m_sc[h] = m_new

    @pl.when(ki == nk - 1)
    def _():
        for h in range(ML_HEADS):
            o_ref[:, h * ML_V:(h + 1) * ML_V] = acc_sc[h] / l_sc[h]


def _attention(q, kv, kpe, q_off):
    B, T, QW = q.shape
    S = kv.shape[1]
    tq = min(256, T)
    tk = 512 if S % 512 == 0 else S
    nk = S // tk
    scale = (ML_NOPE + ML_ROPE) ** -0.5

    def kmap(b, i, j):
        last = ((q_off + (i + 1) * tq - 1) // CHUNK * CHUNK + CHUNK - 1) // tk
        return (b, jnp.minimum(j, last), 0)

    return pl.pallas_call(
        functools.partial(_attn_kernel, tq=tq, tk=tk, q_off=q_off, scale=scale),
        out_shape=jax.ShapeDtypeStruct((B, T, ML_HEADS * ML_V), F32),
        grid=(B, T // tq, nk),
        in_specs=[pl.BlockSpec((None, tq, QW), lambda b, i, j: (b, i, 0)),
                  pl.BlockSpec((None, tk, kv.shape[2]), kmap),
                  pl.BlockSpec((None, tk, ML_ROPE), kmap)],
        out_specs=pl.BlockSpec((None, tq, ML_HEADS * ML_V), lambda b, i, j: (b, i, 0)),
        scratch_shapes=[pltpu.VMEM((ML_HEADS, tq, 1), F32), pltpu.VMEM((ML_HEADS, tq, 1), F32),
                        pltpu.VMEM((ML_HEADS, tq, ML_V), F32)],
        compiler_params=_params("parallel", "parallel", "arbitrary"),
        name="mla_attention",
    )(q, kv, kpe)


def _rms(x, g):
    return x * lax.rsqrt(jnp.mean(x * x, -1, keepdims=True) + RMS_EPS) * g


def _l2n(x):
    return x * lax.rsqrt(jnp.sum(x * x, -1, keepdims=True) + 1e-6)


def _rope(x, pos):
    half = ML_ROPE // 2
    inv = ROPE_THETA ** (-jnp.arange(half, dtype=F32) / half)
    ang = pos.astype(F32)[:, None] * inv
    ang = ang.reshape(ang.shape[0], *([1] * (x.ndim - 3)), half)
    cos, sin = jnp.cos(ang), jnp.sin(ang)
    x1, x2 = jnp.split(x, 2, axis=-1)
    return jnp.concatenate([x1 * cos - x2 * sin, x1 * sin + x2 * cos], -1)


def _small_mm(x2d, w, layer):
    return _mm(x2d, w, layer, tm=1024, tn=1024, tk=w.shape[-2])


def _rwkv_branch(p, shift_prev, S0, lw, layer):
    B, T, _ = p.shape
    W = BRANCH_WIDTH
    prev = jnp.concatenate([shift_prev, p[:, :-1]], axis=1)
    pm = p + (prev - p) * lw["rw_mu"][layer]
    r, k, v = pm[..., :W], pm[..., W:2 * W], pm[..., 2 * W:3 * W]
    o1 = 3 * W
    pw = pm[..., o1:o1 + RW_DECAY_LORA]
    pa = pm[..., o1 + RW_DECAY_LORA:o1 + RW_DECAY_LORA + RW_ICLR_LORA]
    pg = pm[..., o1 + RW_DECAY_LORA + RW_ICLR_LORA:]
    flat = lambda a: a.reshape(B * T, a.shape[-1])
    w_lora = _small_mm(flat(jnp.tanh(pw)), lw["rw_w2"], layer).reshape(B, T, W)
    a_lora = _small_mm(flat(pa), lw["rw_a2"], layer).reshape(B, T, W)
    g = _small_mm(flat(jax.nn.sigmoid(pg)), lw["rw_g2"], layer).reshape(B, T, W)
    w_log = -jax.nn.softplus(-(lw["rw_w0"][layer] + w_lora)) - 0.5
    log_decay = -jnp.exp(w_log)
    a = jax.nn.sigmoid(lw["rw_a0"][layer] + a_lora)
    hd = lambda t: t.reshape(B, T, RW_HEADS, RW_HEAD_DIM)
    kk = _l2n(hd(k * lw["rw_k_k"][layer])).reshape(B, T, W)
    k2 = k * (1.0 + (a - 1.0) * lw["rw_k_a"][layer])
    o, S_T = _rwkv_scan(r, log_decay, k2, v, kk, a, S0)
    o = hd(o)
    mean = jnp.mean(o, -1, keepdims=True)
    var = jnp.mean(jnp.square(o - mean), -1, keepdims=True)
    o = ((o - mean) * lax.rsqrt(var + RW_GN_EPS)).reshape(B, T, W) * lw["rw_lnx_g"][layer] + lw["rw_lnx_b"][layer]
    bonus = jnp.sum(hd(r) * hd(k2) * lw["rw_r_k"][layer], -1, keepdims=True) * hd(v)
    o = (o + bonus.reshape(B, T, W)) * g
    return o, p[:, -1:], S_T


def _gdn_branch(p, conv_prev, S0, lw, layer):
    B, T, _ = p.shape
    qkv = p[..., :GD_CONV_CH]
    z = p[..., GD_CONV_CH:GD_CONV_CH + GD_VW]
    b = p[..., GD_CONV_CH + GD_VW:GD_CONV_CH + GD_VW + GD_V_HEADS]
    a = p[..., GD_CONV_CH + GD_VW + GD_V_HEADS:]
    w = lw["gd_conv_w"][layer]
    xp = jnp.concatenate([conv_prev, qkv], axis=1)
    y = w[0] * xp[:, 0:T]
    for j in range(1, GD_CONV):
        y = y + w[j] * xp[:, j:j + T]
    conv_new = xp[:, -(GD_CONV - 1):]
    y = jax.nn.silu(y)
    q = _l2n(y[..., :GD_QK].reshape(B, T, GD_QK_HEADS, GD_HEAD_DIM)) * GD_HEAD_DIM ** -0.5
    k = _l2n(y[..., GD_QK:2 * GD_QK].reshape(B, T, GD_QK_HEADS, GD_HEAD_DIM))
    v = y[..., 2 * GD_QK:]
    beta = jax.nn.sigmoid(b)
    g = -jnp.exp(lw["gd_a_log"][layer]) * jax.nn.softplus(a + lw["gd_dt_bias"][layer])
    o, S_T = _gdn_scan(q.reshape(B, T, GD_QK), k.reshape(B, T, GD_QK), v, g, beta, S0)
    o = o.reshape(B, T, GD_V_HEADS, GD_HEAD_DIM)
    o = _rms(o, lw["gd_norm_g"][layer]) * jax.nn.silu(z.reshape(B, T, GD_V_HEADS, GD_HEAD_DIM))
    return o.reshape(B, T, GD_VW), conv_new, S_T


def _mla_branch(p, q_off, ckv_past, kpe_past, lw, layer):
    B, T, _ = p.shape
    pos = q_off + jnp.arange(T, dtype=jnp.int32)
    cq = p[..., :ML_Q_RANK]
    ckv = p[..., ML_Q_RANK:ML_Q_RANK + ML_KV_RANK]
    kpe = p[..., ML_Q_RANK + ML_KV_RANK:]
    cqn = _rms(cq, lw["ml_q_norm_g"][layer]).reshape(B * T, ML_Q_RANK)
    q = _mm(cqn, lw["ml_w_uq_perm"], layer, tm=1024, tn=1536, tk=1024).reshape(B, T, -1)
    q_pe = _rope(q[..., ML_HEADS * ML_NOPE:].reshape(B, T, ML_HEADS, ML_ROPE), pos)
    q = jnp.concatenate([q[..., :ML_HEADS * ML_NOPE], q_pe.reshape(B, T, ML_HEADS * ML_ROPE)], -1)
    ckv = _rms(ckv, lw["ml_kv_norm_g"][layer])
    kpe = _rope(kpe, pos)
    if ckv_past is None:
        ckv_all, kpe_all = ckv, kpe
    else:
        ckv_all = jnp.concatenate([ckv_past, ckv], axis=1)
        kpe_all = jnp.concatenate([kpe_past, kpe], axis=1)
    S = ckv_all.shape[1]
    kv = _mm(ckv_all.reshape(B * S, ML_KV_RANK), lw["ml_w_ukv"], layer, tm=1024, tn=2048, tk=512)
    o = _attention(q, kv.reshape(B, S, -1), kpe_all, q_off)
    return o, ckv, kpe


def kernel(x_prompt, x_sample, cache_ckv, cache_kpe, state_rwkv, state_rwkv_shift, state_gdn, state_gdn_conv, w_in, rw_mu, rw_w0, rw_w2, rw_a0, rw_a2, rw_g2, rw_k_k, rw_k_a, rw_r_k, rw_lnx_g, rw_lnx_b, gd_conv_w, gd_a_log, gd_dt_bias, gd_norm_g, ml_q_norm_g, ml_w_uq, ml_kv_norm_g, ml_w_ukv, w_branch, w_out, ln1_g, ln1_b, w_router, router_bias, w_gate, w_up, w_down, ws_gate, ws_up, ws_down, ln2_g, ln2_b):
    depth = w_in.shape[0]
    Bp, Tp, D = x_prompt.shape
    Bs, Ts, _ = x_sample.shape
    P = cache_ckv.shape[2]
    np_, ns = Bp * Tp, Bs * Ts
    alpha = (2 * depth) ** 0.25
    hq = ML_NOPE + ML_ROPE
    cols = jnp.arange(ML_HEADS * hq).reshape(ML_HEADS, hq)
    perm = jnp.concatenate([cols[:, :ML_NOPE].reshape(-1), cols[:, ML_NOPE:].reshape(-1)])
    lw = dict(rw_mu=rw_mu, rw_w0=rw_w0, rw_w2=rw_w2, rw_a0=rw_a0, rw_a2=rw_a2, rw_g2=rw_g2, rw_k_k=rw_k_k,
              rw_k_a=rw_k_a, rw_r_k=rw_r_k, rw_lnx_g=rw_lnx_g, rw_lnx_b=rw_lnx_b, gd_conv_w=gd_conv_w,
              gd_a_log=gd_a_log, gd_dt_bias=gd_dt_bias, gd_norm_g=gd_norm_g, ml_q_norm_g=ml_q_norm_g,
              ml_w_uq_perm=jnp.take(ml_w_uq, perm, axis=2), ml_kv_norm_g=ml_kv_norm_g, ml_w_ukv=ml_w_ukv)
    ws_gu = jnp.concatenate([ws_gate, ws_up], axis=2)

    x = jnp.concatenate([x_prompt.reshape(np_, D), x_sample.reshape(ns, D)], axis=0)
    x16 = x.astype(BF16)
    zeros = lambda *s: jnp.zeros(s, F32)
    o0 = GATE_COLS
    o1 = o0 + RW_COLS
    o2 = o1 + GD_COLS
    states_p, states_s = [], []
    for l in range(depth):
        p = _mm(x16, w_in, l, tm=1536, tn=2048, tk=512)
        pp = p[:np_].reshape(Bp, Tp, IN_COLS)
        ps = p[np_:].reshape(Bs, Ts, IN_COLS)
        rw_p = _rwkv_branch(pp[..., o0:o1], zeros(Bp, 1, RW_COLS), zeros(Bp, RW_HEADS, RW_HEAD_DIM, RW_HEAD_DIM), lw, l)
        rw_s = _rwkv_branch(ps[..., o0:o1], state_rwkv_shift[l], state_rwkv[l], lw, l)
        gd_p = _gdn_branch(pp[..., o1:o2], zeros(Bp, GD_CONV - 1, GD_CONV_CH),
                           zeros(Bp, GD_V_HEADS, GD_HEAD_DIM, GD_HEAD_DIM), lw, l)
        gd_s = _gdn_branch(ps[..., o1:o2], state_gdn_conv[l], state_gdn[l], lw, l)
        ml_p = _mla_branch(pp[..., o2:], 0, None, None, lw, l)
        ml_s = _mla_branch(ps[..., o2:], P, cache_ckv[l], cache_kpe[l], lw, l)
        cat = lambda a, b: jnp.concatenate([a.reshape(np_, -1), b.reshape(ns, -1)], axis=0)
        merged = _merge(p, cat(rw_p[0], rw_s[0]), cat(gd_p[0], gd_s[0]), cat(ml_p[0], ml_s[0]), w_branch, l)
        mix = _mm(merged, w_out, l, tm=1536, tn=2048, tk=512)
        x, x16 = _add_ln(x, mix, ln1_g[l], ln1_b[l], alpha)
        y = _moe(x, x16, w_router, router_bias, w_gate, w_up, w_down, ws_gu, ws_down, l)
        x, x16 = _add_ln(x, y, ln2_g[l], ln2_b[l], alpha)
        states_p.append((ml_p[1], ml_p[2], rw_p[2], rw_p[1], gd_p[2], gd_p[1]))
        states_s.append((ml_s[1], ml_s[2], rw_s[2], rw_s[1], gd_s[2], gd_s[1]))
    sp = tuple(jnp.stack(t) for t in zip(*states_p))
    ss = tuple(jnp.stack(t) for t in zip(*states_s))
    return (x[:np_].reshape(Bp, Tp, D), x[np_:].reshape(Bs, Ts, D)) + sp + ss
```

```python
import functools
import math

import jax
import jax.numpy as jnp
from jax import lax
from jax.experimental import pallas as pl
from jax.experimental.pallas import tpu as pltpu

F32 = jnp.float32
BF16 = jnp.bfloat16

D_MODEL = 4096
CHUNK = 64
BRANCH_WIDTH = 1024
N_BRANCHES = 3
RW_HEADS = 16
RW_HEAD_DIM = 64
RW_DECAY_LORA = 64
RW_ICLR_LORA = 64
RW_GATE_LORA = 128
RW_GN_EPS = 64e-5
RW_COLS = 3 * BRANCH_WIDTH + RW_DECAY_LORA + RW_ICLR_LORA + RW_GATE_LORA
GD_QK_HEADS = 4
GD_V_HEADS = 8
GD_HEAD_DIM = 128
GD_CONV = 4
GD_QK = GD_QK_HEADS * GD_HEAD_DIM
GD_VW = GD_V_HEADS * GD_HEAD_DIM
GD_CONV_CH = 2 * GD_QK + GD_VW
GD_COLS = GD_CONV_CH + GD_VW + 2 * GD_V_HEADS
ML_HEADS = 8
ML_Q_RANK = 1024
ML_KV_RANK = 512
ML_NOPE = 128
ML_ROPE = 64
ML_V = 128
ML_COLS = ML_Q_RANK + ML_KV_RANK + ML_ROPE
ROPE_THETA = 10000.0
GATE_COLS = N_BRANCHES * D_MODEL
IN_COLS = GATE_COLS + RW_COLS + GD_COLS + ML_COLS
N_EXPERTS = 64
TOP_K = 8
EXPERT_FF = 512
SHARED_FF = 512
ROUTED_SCALE = 2.5
LN_EPS = 1e-5
RMS_EPS = 1e-6
NEG_INF = -1e30

VMEM_LIMIT_BYTES = 56 * 1024 * 1024
INV_BLOCK = 16
MOE_TILE = 512


def _params(*sem):
    return pltpu.CompilerParams(dimension_semantics=sem, vmem_limit_bytes=VMEM_LIMIT_BYTES)


def _dot(a, b):
    return jnp.dot(a.astype(BF16), b.astype(BF16), preferred_element_type=F32)


def _dot_nt(a, b):
    return lax.dot_general(a.astype(BF16), b.astype(BF16), (((1,), (1,)), ((), ())),
                           preferred_element_type=F32)


def _dot_tn(a, b):
    return lax.dot_general(a.astype(BF16), b.astype(BF16), (((0,), (0,)), ((), ())),
                           preferred_element_type=F32)


def _split3(a):
    h = a.astype(BF16)
    r = a - h.astype(F32)
    m = r.astype(BF16)
    l = (r - m.astype(F32)).astype(BF16)
    return h, m, l


def _dot_exact_rhs(a, b_bf16):
    h, m, l = _split3(a)
    d = lambda x: jnp.dot(x, b_bf16, preferred_element_type=F32)
    return d(h) + d(m) + d(l)


def _dot_exact_lhs(a_bf16, b):
    h, m, l = _split3(b)
    d = lambda x: jnp.dot(a_bf16, x, preferred_element_type=F32)
    return d(h) + d(m) + d(l)


def _dot_hi(a, b):
    ah = a.astype(BF16)
    al = (a - ah.astype(F32)).astype(BF16)
    bh = b.astype(BF16)
    bl = (b - bh.astype(F32)).astype(BF16)
    d = lambda x, y: jnp.dot(x, y, preferred_element_type=F32)
    return d(ah, bh) + d(ah, bl) + d(al, bh)


def _iota2(c, d=None):
    d = c if d is None else d
    return (lax.broadcasted_iota(jnp.int32, (c, d), 0), lax.broadcasted_iota(jnp.int32, (c, d), 1))


def _unit_lower_inverse(L, c):
    row, col = _iota2(c)
    eye = (row == col).astype(F32)
    same = (row // INV_BLOCK) == (col // INV_BLOCK)
    Ld = jnp.where(same, L, 0.0)
    x = eye - Ld
    p = _dot(Ld, Ld)
    x = x + _dot(x, p)
    p = _dot(p, p)
    x = x + _dot(x, p)
    p = _dot(p, p)
    x = x + _dot(x, p)
    if c == INV_BLOCK:
        return x
    assert c <= 4 * INV_BLOCK
    Lo = L - Ld
    m = _dot(x, Lo)
    m2 = _dot(m, m)
    y = (eye - m) + _dot(eye - m, m2)
    return _dot(y, x)


def _mm_kernel(x_ref, w_ref, o_ref, *, nk):
    d = _dot(x_ref[...], w_ref[...])
    if nk == 1:
        o_ref[...] = d
    else:
        k = pl.program_id(2)

        @pl.when(k == 0)
        def _():
            o_ref[...] = d

        @pl.when(k > 0)
        def _():
            o_ref[...] += d


def _mm(x, w, layer=None, *, tm, tn, tk, xk0=0):
    M = x.shape[0]
    K, N = w.shape[-2:]
    tm = min(tm, M)
    tn = min(tn, N)
    tk = min(tk, K)
    assert K % tk == 0
    nk = K // tk
    if w.ndim == 3:
        w_spec = pl.BlockSpec((None, tk, tn), lambda i, j, k: (layer, k, j))
    else:
        w_spec = pl.BlockSpec((tk, tn), lambda i, j, k: (k, j))
    return pl.pallas_call(
        functools.partial(_mm_kernel, nk=nk),
        out_shape=jax.ShapeDtypeStruct((M, N), F32),
        grid=(pl.cdiv(M, tm), pl.cdiv(N, tn), nk),
        in_specs=[pl.BlockSpec((tm, tk), lambda i, j, k: (i, k + xk0)), w_spec],
        out_specs=pl.BlockSpec((tm, tn), lambda i, j, k: (i, j)),
        compiler_params=_params("parallel", "parallel", "arbitrary"),
        name="mm",
    )(x, w)


def _ln_kernel(x_ref, r_ref, g_ref, b_ref, o_ref, o16_ref, *, alpha):
    z = alpha * x_ref[...] + r_ref[...]
    mu = jnp.mean(z, -1, keepdims=True)
    zc = z - mu
    var = jnp.mean(zc * zc, -1, keepdims=True)
    y = zc * lax.rsqrt(var + LN_EPS) * g_ref[...] + b_ref[...]
    o_ref[...] = y
    o16_ref[...] = y.astype(BF16)


def _add_ln(x, r, g, b, alpha, *, tm=256):
    M, D = x.shape
    tm = min(tm, M)
    row = pl.BlockSpec((tm, D), lambda i: (i, 0))
    vec = pl.BlockSpec((1, D), lambda i: (0, 0))
    return pl.pallas_call(
        functools.partial(_ln_kernel, alpha=alpha),
        out_shape=(jax.ShapeDtypeStruct((M, D), F32), jax.ShapeDtypeStruct((M, D), BF16)),
        grid=(pl.cdiv(M, tm),),
        in_specs=[row, row, vec, vec],
        out_specs=(row, row),
        compiler_params=_params("parallel"),
        name="add_ln",
    )(x, r, g.reshape(1, D), b.reshape(1, D))


def _merge_kernel(g0_ref, g1_ref, g2_ref, o0_ref, o1_ref, o2_ref, w_ref, out_ref):
    acc = jax.nn.sigmoid(g0_ref[...]) * _dot(o0_ref[...], w_ref[0])
    acc += jax.nn.sigmoid(g1_ref[...]) * _dot(o1_ref[...], w_ref[1])
    acc += jax.nn.sigmoid(g2_ref[...]) * _dot(o2_ref[...], w_ref[2])
    out_ref[...] = acc


def _merge(p, o_rw, o_gd, o_ml, w_branch, layer, *, tm=512, tn=512):
    M = p.shape[0]
    D = D_MODEL
    W = BRANCH_WIDTH
    tm = min(tm, M)
    nb = D // tn
    gate = lambda b: pl.BlockSpec((tm, tn), lambda i, j: (i, b * nb + j))
    o_spec = pl.BlockSpec((tm, W), lambda i, j: (i, 0))
    return pl.pallas_call(
        _merge_kernel,
        out_shape=jax.ShapeDtypeStruct((M, D), F32),
        grid=(pl.cdiv(M, tm), nb),
        in_specs=[gate(0), gate(1), gate(2), o_spec, o_spec, o_spec,
                  pl.BlockSpec((None, N_BRANCHES, W, tn), lambda i, j: (layer, 0, 0, j))],
        out_specs=pl.BlockSpec((tm, tn), lambda i, j: (i, j)),
        compiler_params=_params("parallel", "parallel"),
        name="branch_merge",
    )(p, p, p, o_rw, o_gd, o_ml, w_branch)


def _router_kernel(x_ref, w_ref, b_ref, idx_ref, wts_ref):
    logits = _dot_hi(x_ref[...], w_ref[...])
    scores = jax.nn.sigmoid(logits)
    s = scores + b_ref[...]
    tm, E = s.shape
    lane = lax.broadcasted_iota(jnp.int32, (tm, E), 1)
    kcol = lax.broadcasted_iota(jnp.int32, (tm, TOP_K), 1)
    idx = jnp.zeros((tm, TOP_K), jnp.int32)
    sel = jnp.zeros((tm, TOP_K), F32)
    for r in range(TOP_K):
        m = jnp.max(s, -1, keepdims=True)
        pick = jnp.min(jnp.where(s == m, lane, E), -1, keepdims=True)
        hit = lane == pick
        val = jnp.sum(jnp.where(hit, scores, 0.0), -1, keepdims=True)
        idx = jnp.where(kcol == r, pick, idx)
        sel = jnp.where(kcol == r, val, sel)
        s = jnp.where(hit, -jnp.inf, s)
    idx_ref[...] = idx
    wts_ref[...] = sel / jnp.sum(sel, -1, keepdims=True) * ROUTED_SCALE


def _router(x, w_router, router_bias, layer, *, tm=512):
    M, D = x.shape
    E = N_EXPERTS
    tm = min(tm, M)
    return pl.pallas_call(
        _router_kernel,
        out_shape=(jax.ShapeDtypeStruct((M, TOP_K), jnp.int32), jax.ShapeDtypeStruct((M, TOP_K), F32)),
        grid=(pl.cdiv(M, tm),),
        in_specs=[pl.BlockSpec((tm, D), lambda i: (i, 0)),
                  pl.BlockSpec((None, D, E), lambda i: (layer, 0, 0)),
                  pl.BlockSpec((None, 1, E), lambda i: (layer, 0, 0))],
        out_specs=(pl.BlockSpec((tm, TOP_K), lambda i: (i, 0)), pl.BlockSpec((tm, TOP_K), lambda i: (i, 0))),
        compiler_params=_params("parallel"),
        name="router",
    )(x, w_router, router_bias.reshape(router_bias.shape[0], 1, E))


def _expert_up_kernel(te_ref, nu_ref, x_ref, wg_ref, wu_ref, rw_ref, h_ref):
    t = pl.program_id(0)

    @pl.when(t < nu_ref[0])
    def _():
        x = x_ref[...]
        g = _dot(x, wg_ref[...])
        u = _dot(x, wu_ref[...])
        h_ref[...] = (jax.nn.silu(g) * u * rw_ref[...]).astype(h_ref.dtype)


def _expert_down_kernel(te_ref, nu_ref, h_ref, wd_ref, y_ref):
    t = pl.program_id(0)

    @pl.when(t < nu_ref[0])
    def _():
        y_ref[...] = _dot(h_ref[...], wd_ref[...])


def _experts(xs, row_w, tile_e, n_used, w_gate, w_up, w_down, layer, tm):
    R, D = xs.shape
    F = EXPERT_FF
    nt = R // tm
    h = pl.pallas_call(
        _expert_up_kernel,
        out_shape=jax.ShapeDtypeStruct((R, F), BF16),
        grid_spec=pltpu.PrefetchScalarGridSpec(
            num_scalar_prefetch=2,
            grid=(nt,),
            in_specs=[pl.BlockSpec((tm, D), lambda t, te, nu: (t, 0)),
                      pl.BlockSpec((None, None, D, F), lambda t, te, nu: (layer, te[t], 0, 0)),
                      pl.BlockSpec((None, None, D, F), lambda t, te, nu: (layer, te[t], 0, 0)),
                      pl.BlockSpec((tm, 1), lambda t, te, nu: (t, 0))],
            out_specs=pl.BlockSpec((tm, F), lambda t, te, nu: (t, 0)),
        ),
        compiler_params=_params("arbitrary"),
        name="expert_up",
    )(tile_e, n_used, xs, w_gate, w_up, row_w)
    return pl.pallas_call(
        _expert_down_kernel,
        out_shape=jax.ShapeDtypeStruct((R, D), F32),
        grid_spec=pltpu.PrefetchScalarGridSpec(
            num_scalar_prefetch=2,
            grid=(nt,),
            in_specs=[pl.BlockSpec((tm, F), lambda t, te, nu: (t, 0)),
                      pl.BlockSpec((None, None, F, D), lambda t, te, nu: (layer, te[t], 0, 0))],
            out_specs=pl.BlockSpec((tm, D), lambda t, te, nu: (t, 0)),
        ),
        compiler_params=_params("arbitrary"),
        name="expert_down",
    )(tile_e, n_used, h, w_down)


def _shared_act_kernel(gu_ref, o_ref):
    F = SHARED_FF
    gu = gu_ref[...]
    o_ref[...] = jax.nn.silu(gu[:, :F]) * gu[:, F:]


def _moe(x, x16, w_router, router_bias, w_gate, w_up, w_down, ws_gu, ws_down, layer):
    n, D = x.shape
    E = N_EXPERTS
    tm = min(MOE_TILE, n)
    idx, wts = _router(x, w_router, router_bias, layer)
    P = n * TOP_K
    e_flat = idx.reshape(P)
    order = jnp.argsort(e_flat, stable=True).astype(jnp.int32)
    sorted_e = e_flat[order]
    counts = jnp.sum((e_flat[:, None] == jnp.arange(E, dtype=jnp.int32)[None, :]).astype(jnp.int32), 0)
    padded = ((counts + tm - 1) // tm) * tm
    pend = jnp.cumsum(padded)
    pstart = pend - padded
    start = jnp.cumsum(counts) - counts
    dest = pstart[sorted_e] + (jnp.arange(P, dtype=jnp.int32) - start[sorted_e])
    R = (P // tm + E) * tm
    row_token = jnp.zeros((R,), jnp.int32).at[dest].set(order // TOP_K)
    row_w = jnp.zeros((R,), F32).at[dest].set(wts.reshape(P)[order]).reshape(R, 1)
    pos = jnp.zeros((P,), jnp.int32).at[order].set(dest).reshape(n, TOP_K)
    n_used = (pend[-1] // tm).astype(jnp.int32)
    tile_start = jnp.arange(R // tm, dtype=jnp.int32) * tm
    tile_e = jnp.searchsorted(pend, jnp.minimum(tile_start, pend[-1] - 1), side="right").astype(jnp.int32)
    tile_e = jnp.minimum(tile_e, E - 1)
    xs = jnp.take(x16, row_token, axis=0)
    ys = _experts(xs, row_w, tile_e, n_used.reshape(1), w_gate, w_up, w_down, layer, tm)
    routed = jnp.sum(jnp.take(ys, pos, axis=0), axis=1)
    gu = _mm(x16, ws_gu, layer, tm=1536, tn=1024, tk=1024)
    hs = pl.pallas_call(
        _shared_act_kernel,
        out_shape=jax.ShapeDtypeStruct((n, SHARED_FF), F32),
        grid=(pl.cdiv(n, 512),),
        in_specs=[pl.BlockSpec((min(512, n), 2 * SHARED_FF), lambda i: (i, 0))],
        out_specs=pl.BlockSpec((min(512, n), SHARED_FF), lambda i: (i, 0)),
        compiler_params=_params("parallel"),
        name="shared_act",
    )(gu)
    shared = _mm(hs, ws_down, layer, tm=1536, tn=1024, tk=512)
    return routed + shared


def _gdn_kernel(q_ref, k_ref, v_ref, g_ref, b_ref, s0_ref, o_ref, sout_ref, s_sc, *, c, nchunk):
    t = pl.program_id(2)
    dk = GD_HEAD_DIM

    @pl.when(t == 0)
    def _():
        s_sc[...] = s0_ref[...]

    row, col = _iota2(c)
    incl = row >= col
    strict = row > col
    tri = incl.astype(F32)
    eye = (row == col).astype(F32)
    r2, c2 = _iota2(c, dk + c)
    rhs = jnp.where(c2 < dk, 1.0, (r2 > c2 - dk).astype(F32)).astype(BF16)

    def chunk(ci, carry):
        r0 = pl.multiple_of(ci * c, c)
        kc = k_ref[pl.ds(r0, c), :]
        qc = q_ref[pl.ds(r0, c), :]
        kk = _dot_nt(kc, kc)
        qk = _dot_nt(qc, kc)
        for hh in range(2):
            g_row = g_ref[hh, ci]
            b_row = b_ref[hh, ci]
            lhs = jnp.concatenate([tri * g_row, eye * b_row], axis=0)
            cs = _dot_exact_rhs(lhs, rhs)
            gc = cs[:c, :dk]
            dmat = cs[:c, dk:]
            bc = cs[c:, :dk]
            gmat = jnp.where(incl, jnp.exp(jnp.where(incl, dmat, 0.0)), 0.0)
            L = jnp.where(strict, kk * bc[:, :c] * gmat, 0.0)
            T = _unit_lower_inverse(L, c)
            aqk = qk * gmat
            vc = v_ref[pl.ds(r0, c), hh * dk:(hh + 1) * dk]
            eg = jnp.exp(gc)
            glast = gc[c - 1:c, :]
            S = s_sc[hh]
            kq = _dot(jnp.concatenate([kc * bc * eg, qc * eg], 0), S)
            vn = _dot(T, vc * bc - kq[:c])
            o = kq[c:] + _dot(aqk, vn)
            o_ref[pl.ds(r0, c), hh * dk:(hh + 1) * dk] = o
            s_sc[hh] = S * jnp.exp(glast) + _dot_tn(kc * jnp.exp(glast - gc), vn)
        return carry

    lax.fori_loop(0, nchunk, chunk, 0)

    @pl.when(t == pl.num_programs(2) - 1)
    def _():
        sout_ref[...] = s_sc[...]


def _gdn_scan(q, k, v, g, beta, S0):
    B, T, _ = q.shape
    c = min(CHUNK, T)
    tb = min(256, T)
    nchunk = tb // c
    H = GD_V_HEADS
    dk = GD_HEAD_DIM
    rows = lambda a: jnp.swapaxes(a, 1, 2).reshape(B, H, T // c, 1, c)
    qk_spec = pl.BlockSpec((None, tb, dk), lambda b, h, t: (b, t, h))
    v_spec = pl.BlockSpec((None, tb, 2 * dk), lambda b, h, t: (b, t, h))
    r_spec = pl.BlockSpec((None, 2, nchunk, 1, c), lambda b, h, t: (b, h, t, 0, 0))
    s_spec = pl.BlockSpec((None, 2, dk, dk), lambda b, h, t: (b, h, 0, 0))
    return pl.pallas_call(
        functools.partial(_gdn_kernel, c=c, nchunk=nchunk),
        out_shape=(jax.ShapeDtypeStruct((B, T, H * dk), F32), jax.ShapeDtypeStruct(S0.shape, F32)),
        grid=(B, GD_QK_HEADS, T // tb),
        in_specs=[qk_spec, qk_spec, v_spec, r_spec, r_spec, s_spec],
        out_specs=(v_spec, s_spec),
        scratch_shapes=[pltpu.VMEM((2, dk, dk), F32)],
        compiler_params=_params("parallel", "parallel", "arbitrary"),
        name="gdn_scan",
    )(q, k, v, rows(g), rows(beta), S0)


def _rwkv_kernel(r_ref, lw_ref, k_ref, v_ref, kk_ref, a_ref, s0_ref, o_ref, sout_ref, s_sc, *, c, nchunk):
    t = pl.program_id(2)
    n = RW_HEAD_DIM

    @pl.when(t == 0)
    def _():
        s_sc[...] = s0_ref[...]

    row, col = _iota2(c)
    incl = row >= col
    strict = row > col
    tri = incl.astype(BF16)
    row2, col2 = _iota2(c, 2 * c)
    incl2 = row2 >= jnp.where(col2 >= c, col2 - c, col2)

    def chunk(ci, carry):
        sl = pl.ds(pl.multiple_of(ci * c, c), c)
        lw = lw_ref[sl, :]
        kk = kk_ref[sl, :]
        ka = kk * a_ref[sl, :]
        kc = k_ref[sl, :]
        cum = _dot_exact_lhs(tri, lw)
        cl = cum[c - 1:c, :]
        winv = jnp.exp(-cum)
        wrem = jnp.exp(cl - cum)
        a_t = -kk * jnp.exp(cum - lw)
        r_t = r_ref[sl, :] * jnp.exp(cum)
        b_t = ka * winv
        k_t = kc * winv
        b_h = ka * wrem
        k_h = kc * wrem
        wc = jnp.exp(cl)
        vc = v_ref[sl, :]
        outs = []
        for hh in range(2):
            hs = slice(hh * n, (hh + 1) * n)
            ar = jnp.concatenate([a_t[:, hs], r_t[:, hs]], 0)
            p = _dot_nt(ar, jnp.concatenate([b_t[:, hs], k_t[:, hs]], 0))
            aab = jnp.where(strict, p[:c, :c], 0.0)
            aak = jnp.where(strict, p[:c, c:], 0.0)
            arbk = jnp.where(incl2, p[c:, :], 0.0)
            T = _unit_lower_inverse(-aab, c)
            S = s_sc[hh]
            vh = vc[:, hs]
            ars = _dot_nt(ar, S)
            u = _dot(T, ars[:c] + _dot(aak, vh))
            uv = jnp.concatenate([u, vh], 0)
            outs.append(ars[c:] + _dot(arbk, uv))
            s_sc[hh] = S * wc[:, hs] + _dot_tn(uv, jnp.concatenate([b_h[:, hs], k_h[:, hs]], 0))
        o_ref[sl, :] = jnp.concatenate(outs, axis=1)
        return carry

    lax.fori_loop(0, nchunk, chunk, 0)

    @pl.when(t == pl.num_programs(2) - 1)
    def _():
        sout_ref[...] = s_sc[...]


def _rwkv_scan(r, lw, k, v, kk, a, S0):
    B, T, W = r.shape
    c = min(CHUNK, T)
    tb = min(256, T)
    nchunk = tb // c
    n = RW_HEAD_DIM
    x_spec = pl.BlockSpec((None, tb, 2 * n), lambda b, h, t: (b, t, h))
    s_spec = pl.BlockSpec((None, 2, n, n), lambda b, h, t: (b, h, 0, 0))
    return pl.pallas_call(
        functools.partial(_rwkv_kernel, c=c, nchunk=nchunk),
        out_shape=(jax.ShapeDtypeStruct((B, T, W), F32), jax.ShapeDtypeStruct(S0.shape, F32)),
        grid=(B, RW_HEADS // 2, T // tb),
        in_specs=[x_spec] * 6 + [s_spec],
        out_specs=(x_spec, s_spec),
        scratch_shapes=[pltpu.VMEM((2, n, n), F32)],
        compiler_params=_params("parallel", "parallel", "arbitrary"),
        name="rwkv_scan",
    )(r, lw, k, v, kk, a, S0)


def _attn_kernel(q_ref, kv_ref, kpe_ref, o_ref, m_sc, l_sc, acc_sc, *, tq, tk, q_off, scale):
    qi = pl.program_id(1)
    ki = pl.program_id(2)
    nk = pl.num_programs(2)

    @pl.when(ki == 0)
    def _():
        m_sc[...] = jnp.full(m_sc.shape, NEG_INF, F32)
        l_sc[...] = jnp.zeros(l_sc.shape, F32)
        acc_sc[...] = jnp.zeros(acc_sc.shape, F32)

    last_q_chunk = (q_off + (qi + 1) * tq - 1) // CHUNK

    @pl.when(ki * tk <= last_q_chunk * CHUNK + CHUNK - 1)
    def _():
        qpos = q_off + qi * tq + lax.broadcasted_iota(jnp.int32, (tq, tk), 0)
        kpos = ki * tk + lax.broadcasted_iota(jnp.int32, (tq, tk), 1)
        mask = (kpos // CHUNK) <= (qpos // CHUNK)
        kpe = kpe_ref[...]
        for h in range(ML_HEADS):
            qn = q_ref[:, h * ML_NOPE:(h + 1) * ML_NOPE]
            qp = q_ref[:, ML_HEADS * ML_NOPE + h * ML_ROPE: ML_HEADS * ML_NOPE + (h + 1) * ML_ROPE]
            kn = kv_ref[:, h * (ML_NOPE + ML_V): h * (ML_NOPE + ML_V) + ML_NOPE]
            vv = kv_ref[:, h * (ML_NOPE + ML_V) + ML_NOPE:(h + 1) * (ML_NOPE + ML_V)]
            s = (_dot_nt(qn, kn) + _dot_nt(qp, kpe)) * scale
            s = jnp.where(mask, s, NEG_INF)
            m_old = m_sc[h]
            m_new = jnp.maximum(m_old, jnp.max(s, -1, keepdims=True))
            alpha = jnp.exp(m_old - m_new)
            p = jnp.exp(s - m_new)
            l_sc[h] = alpha * l_sc[h] + jnp.sum(p, -1, keepdims=True)
            acc_sc[h] = alpha * acc_sc[h] + _dot(p, vv)
            m_sc[h] = m_new

    @pl.when(ki == nk - 1)
    def _():
        for h in range(ML_HEADS):
            o_ref[:, h * ML_V:(h + 1) * ML_V] = acc_sc[h] / l_sc[h]


def _attention(q, kv, kpe, q_off):
    B, T, QW = q.shape
    S = kv.shape[1]
    tq = min(256, T)
    tk = 512 if S % 512 == 0 else S
    nk = S // tk
    scale = (ML_NOPE + ML_ROPE) ** -0.5

    def kmap(b, i, j):
        last = ((q_off + (i + 1) * tq - 1) // CHUNK * CHUNK + CHUNK - 1) // tk
        return (b, jnp.minimum(j, last), 0)

    return pl.pallas_call(
        functools.partial(_attn_kernel, tq=tq, tk=tk, q_off=q_off, scale=scale),
        out_shape=jax.ShapeDtypeStruct((B, T, ML_HEADS * ML_V), F32),
        grid=(B, T // tq, nk),
        in_specs=[pl.BlockSpec((None, tq, QW), lambda b, i, j: (b, i, 0)),
                  pl.BlockSpec((None, tk, kv.shape[2]), kmap),
                  pl.BlockSpec((None, tk, ML_ROPE), kmap)],
        out_specs=pl.BlockSpec((None, tq, ML_HEADS * ML_V), lambda b, i, j: (b, i, 0)),
        scratch_shapes=[pltpu.VMEM((ML_HEADS, tq, 1), F32), pltpu.VMEM((ML_HEADS, tq, 1), F32),
                        pltpu.VMEM((ML_HEADS, tq, ML_V), F32)],
        compiler_params=_params("parallel", "parallel", "arbitrary"),
        name="mla_attention",
    )(q, kv, kpe)


def _rms(x, g):
    return x * lax.rsqrt(jnp.mean(x * x, -1, keepdims=True) + RMS_EPS) * g


def _l2n(x):
    return x * lax.rsqrt(jnp.sum(x * x, -1, keepdims=True) + 1e-6)


def _rope(x, pos):
    half = ML_ROPE // 2
    inv = ROPE_THETA ** (-jnp.arange(half, dtype=F32) / half)
    ang = pos.astype(F32)[:, None] * inv
    ang = ang.reshape(ang.shape[0], *([1] * (x.ndim - 3)), half)
    cos, sin = jnp.cos(ang), jnp.sin(ang)
    x1, x2 = jnp.split(x, 2, axis=-1)
    return jnp.concatenate([x1 * cos - x2 * sin, x1 * sin + x2 * cos], -1)


def _small_mm(x2d, w, layer):
    return _mm(x2d, w, layer, tm=1024, tn=1024, tk=w.shape[-2])


def _rwkv_branch(p, shift_prev, S0, lw, layer):
    B, T, _ = p.shape
    W = BRANCH_WIDTH
    prev = jnp.concatenate([shift_prev, p[:, :-1]], axis=1)
    pm = p + (prev - p) * lw["rw_mu"][layer]
    r, k, v = pm[..., :W], pm[..., W:2 * W], pm[..., 2 * W:3 * W]
    o1 = 3 * W
    pw = pm[..., o1:o1 + RW_DECAY_LORA]
    pa = pm[..., o1 + RW_DECAY_LORA:o1 + RW_DECAY_LORA + RW_ICLR_LORA]
    pg = pm[..., o1 + RW_DECAY_LORA + RW_ICLR_LORA:]
    flat = lambda a: a.reshape(B * T, a.shape[-1])
    w_lora = _small_mm(flat(jnp.tanh(pw)), lw["rw_w2"], layer).reshape(B, T, W)
    a_lora = _small_mm(flat(pa), lw["rw_a2"], layer).reshape(B, T, W)
    g = _small_mm(flat(jax.nn.sigmoid(pg)), lw["rw_g2"], layer).reshape(B, T, W)
    w_log = -jax.nn.softplus(-(lw["rw_w0"][layer] + w_lora)) - 0.5
    log_decay = -jnp.exp(w_log)
    a = jax.nn.sigmoid(lw["rw_a0"][layer] + a_lora)
    hd = lambda t: t.reshape(B, T, RW_HEADS, RW_HEAD_DIM)
    kk = _l2n(hd(k * lw["rw_k_k"][layer])).reshape(B, T, W)
    k2 = k * (1.0 + (a - 1.0) * lw["rw_k_a"][layer])
    o, S_T = _rwkv_scan(r, log_decay, k2, v, kk, a, S0)
    o = hd(o)
    mean = jnp.mean(o, -1, keepdims=True)
    var = jnp.mean(jnp.square(o - mean), -1, keepdims=True)
    o = ((o - mean) * lax.rsqrt(var + RW_GN_EPS)).reshape(B, T, W) * lw["rw_lnx_g"][layer] + lw["rw_lnx_b"][layer]
    bonus = jnp.sum(hd(r) * hd(k2) * lw["rw_r_k"][layer], -1, keepdims=True) * hd(v)
    o = (o + bonus.reshape(B, T, W)) * g
    return o, p[:, -1:], S_T


def _gdn_branch(p, conv_prev, S0, lw, layer):
    B, T, _ = p.shape
    qkv = p[..., :GD_CONV_CH]
    z = p[..., GD_CONV_CH:GD_CONV_CH + GD_VW]
    b = p[..., GD_CONV_CH + GD_VW:GD_CONV_CH + GD_VW + GD_V_HEADS]
    a = p[..., GD_CONV_CH + GD_VW + GD_V_HEADS:]
    w = lw["gd_conv_w"][layer]
    xp = jnp.concatenate([conv_prev, qkv], axis=1)
    y = w[0] * xp[:, 0:T]
    for j in range(1, GD_CONV):
        y = y + w[j] * xp[:, j:j + T]
    conv_new = xp[:, -(GD_CONV - 1):]
    y = jax.nn.silu(y)
    q = _l2n(y[..., :GD_QK].reshape(B, T, GD_QK_HEADS, GD_HEAD_DIM)) * GD_HEAD_DIM ** -0.5
    k = _l2n(y[..., GD_QK:2 * GD_QK].reshape(B, T, GD_QK_HEADS, GD_HEAD_DIM))
    v = y[..., 2 * GD_QK:]
    beta = jax.nn.sigmoid(b)
    g = -jnp.exp(lw["gd_a_log"][layer]) * jax.nn.softplus(a + lw["gd_dt_bias"][layer])
    o, S_T = _gdn_scan(q.reshape(B, T, GD_QK), k.reshape(B, T, GD_QK), v, g, beta, S0)
    o = o.reshape(B, T, GD_V_HEADS, GD_HEAD_DIM)
    o = _rms(o, lw["gd_norm_g"][layer]) * jax.nn.silu(z.reshape(B, T, GD_V_HEADS, GD_HEAD_DIM))
    return o.reshape(B, T, GD_VW), conv_new, S_T


def _mla_branch(p, q_off, ckv_past, kpe_past, lw, layer):
    B, T, _ = p.shape
    pos = q_off + jnp.arange(T, dtype=jnp.int32)
    cq = p[..., :ML_Q_RANK]
    ckv = p[..., ML_Q_RANK:ML_Q_RANK + ML_KV_RANK]
    kpe = p[..., ML_Q_RANK + ML_KV_RANK:]
    cqn = _rms(cq, lw["ml_q_norm_g"][layer]).reshape(B * T, ML_Q_RANK)
    q = _mm(cqn, lw["ml_w_uq_perm"], layer, tm=1024, tn=1536, tk=1024).reshape(B, T, -1)
    q_pe = _rope(q[..., ML_HEADS * ML_NOPE:].reshape(B, T, ML_HEADS, ML_ROPE), pos)
    q = jnp.concatenate([q[..., :ML_HEADS * ML_NOPE], q_pe.reshape(B, T, ML_HEADS * ML_ROPE)], -1)
    ckv = _rms(ckv, lw["ml_kv_norm_g"][layer])
    kpe = _rope(kpe, pos)
    if ckv_past is None:
        ckv_all, kpe_all = ckv, kpe
    else:
        ckv_all = jnp.concatenate([ckv_past, ckv], axis=1)
        kpe_all = jnp.concatenate([kpe_past, kpe], axis=1)
    S = ckv_all.shape[1]
    kv = _mm(ckv_all.reshape(B * S, ML_KV_RANK), lw["ml_w_ukv"], layer, tm=1024, tn=2048, tk=512)
    o = _attention(q, kv.reshape(B, S, -1), kpe_all, q_off)
    return o, ckv, kpe


def kernel(x_prompt, x_sample, cache_ckv, cache_kpe, state_rwkv, state_rwkv_shift, state_gdn, state_gdn_conv, w_in, rw_mu, rw_w0, rw_w2, rw_a0, rw_a2, rw_g2, rw_k_k, rw_k_a, rw_r_k, rw_lnx_g, rw_lnx_b, gd_conv_w, gd_a_log, gd_dt_bias, gd_norm_g, ml_q_norm_g, ml_w_uq, ml_kv_norm_g, ml_w_ukv, w_branch, w_out, ln1_g, ln1_b, w_router, router_bias, w_gate, w_up, w_down, ws_gate, ws_up, ws_down, ln2_g, ln2_b):
    depth = w_in.shape[0]
    Bp, Tp, D = x_prompt.shape
    Bs, Ts, _ = x_sample.shape
    P = cache_ckv.shape[2]
    np_, ns = Bp * Tp, Bs * Ts
    alpha = (2 * depth) ** 0.25
    hq = ML_NOPE + ML_ROPE
    cols = jnp.arange(ML_HEADS * hq).reshape(ML_HEADS, hq)
    perm = jnp.concatenate([cols[:, :ML_NOPE].reshape(-1), cols[:, ML_NOPE:].reshape(-1)])
    lw = dict(rw_mu=rw_mu, rw_w0=rw_w0, rw_w2=rw_w2, rw_a0=rw_a0, rw_a2=rw_a2, rw_g2=rw_g2, rw_k_k=rw_k_k,
              rw_k_a=rw_k_a, rw_r_k=rw_r_k, rw_lnx_g=rw_lnx_g, rw_lnx_b=rw_lnx_b, gd_conv_w=gd_conv_w,
              gd_a_log=gd_a_log, gd_dt_bias=gd_dt_bias, gd_norm_g=gd_norm_g, ml_q_norm_g=ml_q_norm_g,
              ml_w_uq_perm=jnp.take(ml_w_uq, perm, axis=2), ml_kv_norm_g=ml_kv_norm_g, ml_w_ukv=ml_w_ukv)
    ws_gu = jnp.concatenate([ws_gate, ws_up], axis=2)

    x = jnp.concatenate([x_prompt.reshape(np_, D), x_sample.reshape(ns, D)], axis=0)
    x16 = x.astype(BF16)
    zeros = lambda *s: jnp.zeros(s, F32)
    o0 = GATE_COLS
    o1 = o0 + RW_COLS
    o2 = o1 + GD_COLS
    states_p, states_s = [], []
    for l in range(depth):
        p = _mm(x16, w_in, l, tm=1536, tn=2048, tk=512)
        pp = p[:np_].reshape(Bp, Tp, IN_COLS)
        ps = p[np_:].reshape(Bs, Ts, IN_COLS)
        rw_p = _rwkv_branch(pp[..., o0:o1], zeros(Bp, 1, RW_COLS), zeros(Bp, RW_HEADS, RW_HEAD_DIM, RW_HEAD_DIM), lw, l)
        rw_s = _rwkv_branch(ps[..., o0:o1], state_rwkv_shift[l], state_rwkv[l], lw, l)
        gd_p = _gdn_branch(pp[..., o1:o2], zeros(Bp, GD_CONV - 1, GD_CONV_CH),
                           zeros(Bp, GD_V_HEADS, GD_HEAD_DIM, GD_HEAD_DIM), lw, l)
        gd_s = _gdn_branch(ps[..., o1:o2], state_gdn_conv[l], state_gdn[l], lw, l)
        ml_p = _mla_branch(pp[..., o2:], 0, None, None, lw, l)
        ml_s = _mla_branch(ps[..., o2:], P, cache_ckv[l], cache_kpe[l], lw, l)
        cat = lambda a, b: jnp.concatenate([a.reshape(np_, -1), b.reshape(ns, -1)], axis=0)
        merged = _merge(p, cat(rw_p[0], rw_s[0]), cat(gd_p[0], gd_s[0]), cat(ml_p[0], ml_s[0]), w_branch, l)
        mix = _mm(merged, w_out, l, tm=1536, tn=2048, tk=512)
        x, x16 = _add_ln(x, mix, ln1_g[l], ln1_b[l], alpha)
        y = _moe(x, x16, w_router, router_bias, w_gate, w_up, w_down, ws_gu, ws_down, l)
        x, x16 = _add_ln(x, y, ln2_g[l], ln2_b[l], alpha)
        states_p.append((ml_p[1], ml_p[2], rw_p[2], rw_p[1], gd_p[2], gd_p[1]))
        states_s.append((ml_s[1], ml_s[2], rw_s[2], rw_s[1], gd_s[2], gd_s[1]))
    sp = tuple(jnp.stack(t) for t in zip(*states_p))
    ss = tuple(jnp.stack(t) for t in zip(*states_s))
    return (x[:np_].reshape(Bp, Tp, D), x[np_:].reshape(Bs, Ts, D)) + sp + ss
```
